```python
import math
import jax, jax.numpy as jnp
from jax import lax
import numpy as np

D_MODEL = 1024
BATCH = 4
SEQ = 4096
DEPTH = 1

NORM_EPS = 1e-6
N_MOD = 6

RW_HEADS = 8
RW_HEAD_DIM = 64
RW_WIDTH = RW_HEADS * RW_HEAD_DIM
RW_DECAY_LORA = 64
RW_AAA_LORA = 64
RW_GATE_LORA = 128
RW_GN_EPS = 64e-5
RW_COLS = 3 * RW_WIDTH + RW_DECAY_LORA + RW_AAA_LORA + RW_GATE_LORA

NSA_HEADS = 8
NSA_KV_GROUPS = 2
NSA_HPG = NSA_HEADS // NSA_KV_GROUPS
NSA_HEAD_DIM = 64
NSA_WIDTH = NSA_HEADS * NSA_HEAD_DIM
NSA_KV_WIDTH = NSA_KV_GROUPS * NSA_HEAD_DIM
NSA_N_BRANCH = 3
NSA_COLS = NSA_WIDTH + 6 * NSA_KV_WIDTH + NSA_N_BRANCH * NSA_HEADS
CMP_LEN = 32
CMP_STRIDE = 16
CMP_RATIO = CMP_LEN // CMP_STRIDE
CMP_HIDDEN = 256
SEL_BLK = 64
SEL_RATIO = SEL_BLK // CMP_STRIDE
SEL_TOPN = 16
WINDOW = 512
Q_BLK = 128
ROPE_THETA = 10000.0
MASK_NEG = -1e30
SEL_FORCE = 1e9

N_BRANCHES = 2
MERGE_COLS = N_BRANCHES * D_MODEL
IN_COLS = RW_COLS + NSA_COLS + MERGE_COLS

N_EXPERT_GROUPS = 4
EXPERTS_PER_GROUP = 8
N_EXPERTS = N_EXPERT_GROUPS * EXPERTS_PER_GROUP
EXPERT_TOPK = 2
EXPERT_HIDDEN = 512
MOE_BLK = 128

kernel_name = 'hybrid_rwkv7_nsa_hmoe_block'


def rmsnorm(x, g):
    xf = x.astype(jnp.float32)
    y = xf * lax.rsqrt(jnp.mean(xf * xf, axis=-1, keepdims=True) + NORM_EPS)
    return (y * g.astype(jnp.float32)).astype(x.dtype)


def rope(x, pos):
    half = x.shape[-1] // 2
    inv = jnp.exp(-math.log(ROPE_THETA) * jnp.arange(half, dtype=jnp.float32) / half)
    ang = pos.astype(jnp.float32)[:, None] * inv[None, :]
    cos = jnp.cos(ang)[None, :, None, :]
    sin = jnp.sin(ang)[None, :, None, :]
    xf = x.astype(jnp.float32)
    x1, x2 = xf[..., :half], xf[..., half:]
    return jnp.concatenate([x1 * cos - x2 * sin, x2 * cos + x1 * sin], axis=-1).astype(x.dtype)


def masked_softmax(s, mask):
    s = jnp.where(mask, s.astype(jnp.float32), MASK_NEG)
    return jax.nn.softmax(s, axis=-1) * mask


def token_shift(z):
    return jnp.pad(z, ((0, 0), (1, 0), (0, 0)))[:, :-1]


def rwkv7_time_mix(zr, mu, w0, w2, a0, a2, g2, k_k, k_a, r_k, ln_w, ln_b):
    B, T, _ = zr.shape
    f32 = jnp.float32
    zr = zr + (token_shift(zr) - zr) * mu
    r, k, v, wl, al, gl = jnp.split(zr, [RW_WIDTH, 2 * RW_WIDTH, 3 * RW_WIDTH,
                                         3 * RW_WIDTH + RW_DECAY_LORA,
                                         3 * RW_WIDTH + RW_DECAY_LORA + RW_AAA_LORA], axis=-1)
    w_log = -jax.nn.softplus(-(w0 + jnp.tanh(wl) @ w2)) - 0.5
    decay = jnp.exp(-jnp.exp(w_log.astype(f32)))
    a = jax.nn.sigmoid(a0 + al @ a2)
    g = jax.nn.sigmoid(gl) @ g2
    heads = lambda t: t.reshape(B, T, RW_HEADS, RW_HEAD_DIM).astype(f32)
    kk = heads(k * k_k)
    kk = kk / jnp.maximum(jnp.sqrt(jnp.sum(kk * kk, axis=-1, keepdims=True)), 1e-12)
    k = k * (1 + (a - 1) * k_a)
    rh, kh, vh, ah, dh = heads(r), heads(k), heads(v), heads(a), heads(decay)
    xs = tuple(jnp.moveaxis(t, 1, 0) for t in (rh, dh, kh, vh, -kk, kk * ah))

    def step(S, inp):
        r_t, w_t, k_t, v_t, a_t, b_t = inp
        sa = jnp.einsum('bhvk,bhk->bhv', S, a_t)
        S = S * w_t[:, :, None, :] + sa[..., None] * b_t[:, :, None, :] + v_t[..., None] * k_t[:, :, None, :]
        return S, jnp.einsum('bhvk,bhk->bhv', S, r_t)

    S0 = jnp.zeros((B, RW_HEADS, RW_HEAD_DIM, RW_HEAD_DIM), f32)
    _, ys = lax.scan(step, S0, xs)
    y = jnp.moveaxis(ys, 0, 1)
    mean = jnp.mean(y, axis=-1, keepdims=True)
    var = jnp.mean(jnp.square(y - mean), axis=-1, keepdims=True)
    y = ((y - mean) * lax.rsqrt(var + RW_GN_EPS)).reshape(B, T, RW_WIDTH) * ln_w + ln_b
    bonus = jnp.sum(rh * kh * r_k, axis=-1, keepdims=True) * vh
    y = y + bonus.reshape(B, T, RW_WIDTH)
    return (y * g).astype(zr.dtype)


def nsa_mix(zn, pos, pe_k, w1_k, w2_k, pe_v, w1_v, w2_v):
    B, T, _ = zn.shape
    G, HPG, DK = NSA_KV_GROUPS, NSA_HPG, NSA_HEAD_DIM
    q, kc, vc, ks, vs, kw, vw, gl = jnp.split(
        zn, [NSA_WIDTH + i * NSA_KV_WIDTH for i in range(7)], axis=-1)
    kv = lambda t: t.reshape(B, T, G, DK)
    q = rope(q.reshape(B, T, NSA_HEADS, DK), pos)
    kc, vc, vs, vw = kv(kc), kv(vc), kv(vs), kv(vw)
    ks = rope(kv(ks), pos)
    kw = rope(kv(kw), pos)
    gates = jax.nn.sigmoid(gl.astype(jnp.float32)).reshape(B, T, NSA_HEADS, NSA_N_BRANCH)

    n_cmp = (T - CMP_LEN) // CMP_STRIDE + 1
    cmp_idx = jnp.arange(n_cmp)[:, None] * CMP_STRIDE + jnp.arange(CMP_LEN)[None, :]
    cmp_end = jnp.arange(n_cmp) * CMP_STRIDE + CMP_LEN - 1

    def compress(t, pe, w1, w2):
        blocks = t[:, cmp_idx] + pe[None, None, :, None, :]
        hid = jax.nn.gelu(jnp.einsum('bnlgd,ldf->bngf', blocks, w1.reshape(CMP_LEN, DK, CMP_HIDDEN)))
        return jnp.einsum('bngf,fd->bngd', hid, w2)

    kc_c = rope(compress(kc, pe_k, w1_k, w2_k), cmp_end)
    vc_c = compress(vc, pe_v, w1_v, w2_v)

    n_sel = T // SEL_BLK
    top_n = min(SEL_TOPN, n_sel)
    remap_idx = jnp.arange(n_sel)[:, None] * SEL_RATIO + jnp.arange(SEL_RATIO + CMP_RATIO - 1)[None, :]
    remap_w = jnp.asarray(np.convolve(np.ones(SEL_RATIO), np.ones(CMP_RATIO)), jnp.float32)
    right_pad = max(0, n_sel * SEL_RATIO - n_cmp)
    jsel = jnp.arange(n_sel)
    ks_blk = ks.reshape(B, n_sel, SEL_BLK, G, DK).transpose(0, 3, 1, 2, 4)
    vs_blk = vs.reshape(B, n_sel, SEL_BLK, G, DK).transpose(0, 3, 1, 2, 4)
    bi = jnp.arange(B)[:, None, None, None]
    gi = jnp.arange(G)[None, :, None, None]
    kw_pad = jnp.pad(kw, ((0, 0), (WINDOW, 0), (0, 0), (0, 0)))
    vw_pad = jnp.pad(vw, ((0, 0), (WINDOW, 0), (0, 0), (0, 0)))
    scale = NSA_HEAD_DIM ** -0.5

    def block(qb):
        q0 = qb * Q_BLK
        qpos = q0 + jnp.arange(Q_BLK)
        qg = lax.dynamic_slice_in_dim(q, q0, Q_BLK, axis=1).reshape(B, Q_BLK, G, HPG, DK)
        s_c = jnp.einsum('bqghd,bngd->bghqn', qg, kc_c) * scale
        p_c = masked_softmax(s_c, cmp_end[None, :] <= qpos[:, None])
        o_c = jnp.einsum('bghqn,bngd->bqghd', p_c, vc_c)
        p_pad = jnp.pad(p_c.sum(axis=2), ((0, 0), (0, 0), (0, 0), (CMP_RATIO - 1, right_pad)))
        p_slc = jnp.einsum('bgqjs,s->bgqj', p_pad[..., remap_idx], remap_w)
        qblk = qpos // SEL_BLK
        forced = (jsel[None, :] == 0) | (jsel[None, :] == qblk[:, None]) | (jsel[None, :] == qblk[:, None] - 1)
        future = jsel[None, :] > qblk[:, None]
        score = jnp.where(forced, SEL_FORCE, jnp.where(future, -SEL_FORCE, p_slc))
        _, sel = lax.top_k(score, top_n)
        k_sel = ks_blk[bi, gi, sel]
        v_sel = vs_blk[bi, gi, sel]
        s_s = jnp.einsum('bqghd,bgqnld->bghqnl', qg, k_sel) * scale
        tok = sel[..., None] * SEL_BLK + jnp.arange(SEL_BLK)
        mask_s = (tok <= qpos[None, None, :, None, None]).reshape(B, G, 1, Q_BLK, -1)
        p_s = masked_softmax(s_s.reshape(B, G, HPG, Q_BLK, -1), mask_s).reshape(s_s.shape)
        o_s = jnp.einsum('bghqnl,bgqnld->bqghd', p_s, v_sel)
        kwin = lax.dynamic_slice_in_dim(kw_pad, q0, Q_BLK + WINDOW, axis=1)
        vwin = lax.dynamic_slice_in_dim(vw_pad, q0, Q_BLK + WINDOW, axis=1)
        kpos = q0 - WINDOW + jnp.arange(Q_BLK + WINDOW)
        mask_w = (kpos[None, :] <= qpos[:, None]) & (kpos[None, :] > qpos[:, None] - WINDOW) & (kpos[None, :] >= 0)
        s_w = jnp.einsum('bqghd,bkgd->bghqk', qg, kwin) * scale
        p_w = masked_softmax(s_w, mask_w)
        o_w = jnp.einsum('bghqk,bkgd->bqghd', p_w, vwin)
        gb = lax.dynamic_slice_in_dim(gates, q0, Q_BLK, axis=1).reshape(B, Q_BLK, G, HPG, NSA_N_BRANCH)
        o = gb[..., 0:1] * o_c + gb[..., 1:2] * o_s + gb[..., 2:3] * o_w
        return o.reshape(B, Q_BLK, NSA_WIDTH)

    outs = lax.map(block, jnp.arange(T // Q_BLK))
    return outs.transpose(1, 0, 2, 3).reshape(B, T, NSA_WIDTH).astype(zn.dtype)


def hier_moe(u, w_rg, b_rg, w_re, b_re, w_gate, w_up, w_down):
    B, T, D = u.shape
    N = B * T
    f32 = jnp.float32
    xf = u.reshape(N, D)
    grp_p = jax.nn.softmax((xf @ w_rg).astype(f32) + b_rg, axis=-1)
    grp_w, grp = lax.top_k(grp_p, 1)
    exp_logits = ((xf @ w_re).astype(f32) + b_re).reshape(N, N_EXPERT_GROUPS, EXPERTS_PER_GROUP)
    in_grp = jnp.take_along_axis(exp_logits, grp[:, :, None], axis=1)[:, 0]
    top_v, top_i = lax.top_k(in_grp, EXPERT_TOPK)
    comb = grp_w * jax.nn.softmax(top_v, axis=-1)
    expert_id = grp * EXPERTS_PER_GROUP + top_i
    A = N * EXPERT_TOPK
    eid = expert_id.reshape(A)
    tid = jnp.repeat(jnp.arange(N, dtype=jnp.int32), EXPERT_TOPK)
    wts = comb.reshape(A)
    order = jnp.argsort(eid)
    se = eid[order]
    counts = jnp.bincount(eid, length=N_EXPERTS)
    starts = jnp.cumsum(counts) - counts
    pcounts = (counts + MOE_BLK - 1) // MOE_BLK * MOE_BLK
    pends = jnp.cumsum(pcounts)
    pstarts = pends - pcounts
    dest = pstarts[se] + jnp.arange(A) - starts[se]
    P = -(-A // MOE_BLK) * MOE_BLK + N_EXPERTS * MOE_BLK
    n_blk = P // MOE_BLK
    buf_tok = jnp.full((P,), N, jnp.int32).at[dest].set(tid[order])
    buf_w = jnp.zeros((P,), f32).at[dest].set(wts[order])
    blk_exp = jnp.minimum(jnp.searchsorted(pends, jnp.arange(n_blk) * MOE_BLK, side='right'), N_EXPERTS - 1)
    x_pad = jnp.concatenate([xf, jnp.zeros((1, D), xf.dtype)], axis=0)
    xb = x_pad[buf_tok].reshape(n_blk, MOE_BLK, D)

    def expert_block(args):
        xblk, e = args
        hid = jax.nn.silu(xblk @ w_gate[e]) * (xblk @ w_up[e])
        return hid @ w_down[e]

    yb = lax.map(expert_block, (xb, blk_exp)).reshape(P, D)
    y = jax.ops.segment_sum(yb * buf_w[:, None], buf_tok, num_segments=N + 1)[:N]
    return y.reshape(B, T, D).astype(u.dtype)


def setup_inputs(seed: int = 0):
    key = jax.random.key(seed)
    k = jax.random.split(key, 35)
    f32 = jnp.float32
    L, D = DEPTH, D_MODEL
    nrm = lambda i, shape, s: jax.random.normal(k[i], shape, f32) * s
    uni = lambda i, shape, lo, hi: jax.random.uniform(k[i], shape, f32, lo, hi)
    return {
        'x': nrm(0, (BATCH, SEQ, D), 1.0),
        'c': nrm(1, (BATCH, D), 1.0),
        'w_ada': nrm(2, (L, D, N_MOD * D), D ** -0.5),
        'b_ada': nrm(3, (L, N_MOD * D), 0.01),
        'norm_mix_g': 1.0 + nrm(4, (L, D), 0.02),
        'w_in': nrm(5, (L, D, IN_COLS), D ** -0.5),
        'rwkv_mu': uni(6, (L, RW_COLS), 0.0, 1.0),
        'rwkv_w0': uni(7, (L, RW_WIDTH), -5.0, -1.0),
        'rwkv_w2': nrm(8, (L, RW_DECAY_LORA, RW_WIDTH), 0.1 * RW_DECAY_LORA ** -0.5),
        'rwkv_a0': nrm(9, (L, RW_WIDTH), 0.5),
        'rwkv_a2': nrm(10, (L, RW_AAA_LORA, RW_WIDTH), 0.5 * RW_AAA_LORA ** -0.5),
        'rwkv_g2': nrm(11, (L, RW_GATE_LORA, RW_WIDTH), RW_GATE_LORA ** -0.5),
        'rwkv_kk': 0.85 + nrm(12, (L, RW_WIDTH), 0.05),
        'rwkv_ka': 1.0 + nrm(13, (L, RW_WIDTH), 0.05),
        'rwkv_rk': nrm(14, (L, RW_HEADS, RW_HEAD_DIM), 0.1),
        'rwkv_ln_w': 1.0 + nrm(15, (L, RW_WIDTH), 0.02),
        'rwkv_ln_b': nrm(16, (L, RW_WIDTH), 0.01),
        'cmp_pe_k': nrm(17, (L, CMP_LEN, NSA_HEAD_DIM), 0.1),
        'cmp_w1_k': nrm(18, (L, CMP_LEN * NSA_HEAD_DIM, CMP_HIDDEN), (CMP_LEN * NSA_HEAD_DIM) ** -0.5),
        'cmp_w2_k': nrm(19, (L, CMP_HIDDEN, NSA_HEAD_DIM), CMP_HIDDEN ** -0.5),
        'cmp_pe_v': nrm(20, (L, CMP_LEN, NSA_HEAD_DIM), 0.1),
        'cmp_w1_v': nrm(21, (L, CMP_LEN * NSA_HEAD_DIM, CMP_HIDDEN), (CMP_LEN * NSA_HEAD_DIM) ** -0.5),
        'cmp_w2_v': nrm(22, (L, CMP_HIDDEN, NSA_HEAD_DIM), CMP_HIDDEN ** -0.5),
        'p_rwkv': nrm(23, (L, RW_WIDTH, D), RW_WIDTH ** -0.5),
        'p_nsa': nrm(24, (L, NSA_WIDTH, D), NSA_WIDTH ** -0.5),
        'w_out': nrm(25, (L, D, D), D ** -0.5),
        'norm_ffn_g': 1.0 + nrm(26, (L, D), 0.02),
        'w_router_grp': nrm(27, (L, D, N_EXPERT_GROUPS), D ** -0.5),
        'b_router_grp': nrm(28, (L, N_EXPERT_GROUPS), 0.01),
        'w_router_exp': nrm(29, (L, D, N_EXPERTS), D ** -0.5),
        'b_router_exp': nrm(30, (L, N_EXPERTS), 0.01),
        'moe_w_gate': nrm(31, (L, N_EXPERTS, D, EXPERT_HIDDEN), D ** -0.5),
        'moe_w_up': nrm(32, (L, N_EXPERTS, D, EXPERT_HIDDEN), D ** -0.5),
        'moe_w_down': nrm(33, (L, N_EXPERTS, EXPERT_HIDDEN, D), EXPERT_HIDDEN ** -0.5),
        'norm_final_g': 1.0 + nrm(34, (D,), 0.02),
    }


def reference(x, c, w_ada, b_ada, norm_mix_g, w_in, rwkv_mu, rwkv_w0, rwkv_w2, rwkv_a0, rwkv_a2,
              rwkv_g2, rwkv_kk, rwkv_ka, rwkv_rk, rwkv_ln_w, rwkv_ln_b, cmp_pe_k, cmp_w1_k, cmp_w2_k,
              cmp_pe_v, cmp_w1_v, cmp_w2_v, p_rwkv, p_nsa, w_out, norm_ffn_g, w_router_grp,
              b_router_grp, w_router_exp, b_router_exp, moe_w_gate, moe_w_up, moe_w_down, norm_final_g):
    B, T, D = x.shape
    pos = jnp.arange(T)
    cs = jax.nn.silu(c)
    h = x
    for l in range(DEPTH):
        mod = cs @ w_ada[l] + b_ada[l]
        sh1, sc1, gt1, sh2, sc2, gt2 = jnp.split(mod[:, None, :], N_MOD, axis=-1)
        u = rmsnorm(h, norm_mix_g[l]) * (1 + sc1) + sh1
        z = u @ w_in[l]
        zr, zn, zg = jnp.split(z, [RW_COLS, RW_COLS + NSA_COLS], axis=-1)
        y_r = rwkv7_time_mix(zr, rwkv_mu[l], rwkv_w0[l], rwkv_w2[l], rwkv_a0[l], rwkv_a2[l], rwkv_g2[l],
                             rwkv_kk[l], rwkv_ka[l], rwkv_rk[l], rwkv_ln_w[l], rwkv_ln_b[l])
        y_n = nsa_mix(zn, pos, cmp_pe_k[l], cmp_w1_k[l], cmp_w2_k[l], cmp_pe_v[l], cmp_w1_v[l], cmp_w2_v[l])
        gates = jax.nn.sigmoid(zg).reshape(B, T, N_BRANCHES, D)
        merged = gates[:, :, 0] * (y_r @ p_rwkv[l]) + gates[:, :, 1] * (y_n @ p_nsa[l])
        h = h + gt1 * (merged @ w_out[l])
        u2 = rmsnorm(h, norm_ffn_g[l]) * (1 + sc2) + sh2
        h = h + gt2 * hier_moe(u2, w_router_grp[l], b_router_grp[l], w_router_exp[l], b_router_exp[l],
                               moe_w_gate[l], moe_w_up[l], moe_w_down[l])
    return rmsnorm(h, norm_final_g)
```

```python
import functools
import math

import numpy as np
import jax
import jax.numpy as jnp
from jax import lax
from jax.experimental import pallas as pl
from jax.experimental.pallas import tpu as pltpu

F32 = jnp.float32
BF16 = jnp.bfloat16
HIGHEST = lax.Precision.HIGHEST

NORM_EPS = 1e-6
RW_HEADS = 8
RW_HEAD_DIM = 64
RW_WIDTH = RW_HEADS * RW_HEAD_DIM
RW_DECAY_LORA = 64
RW_AAA_LORA = 64
RW_GATE_LORA = 128
RW_GN_EPS = 64e-5
RW_CHUNK = 64

NSA_HEADS = 8
NSA_KV_GROUPS = 2
NSA_HPG = NSA_HEADS // NSA_KV_GROUPS
NSA_HEAD_DIM = 64
NSA_WIDTH = NSA_HEADS * NSA_HEAD_DIM
NSA_KV_WIDTH = NSA_KV_GROUPS * NSA_HEAD_DIM
NSA_N_BRANCH = 3
CMP_LEN = 32
CMP_STRIDE = 16
CMP_HIDDEN = 256
SEL_BLK = 64
SEL_TOPN = 16
WINDOW = 512
ROPE_THETA = 10000.0
MASK_NEG = -1e30
SEL_FORCE = 1e9

N_EXPERT_GROUPS = 4
EXPERTS_PER_GROUP = 8
N_EXPERTS = N_EXPERT_GROUPS * EXPERTS_PER_GROUP
EXPERT_HIDDEN = 512

LANES = 128
VMEM_LIMIT = 56 * 1024 * 1024

PROJ_TM = 256
ATTN_TQ = 128
ATTN_TK = 512
MERGE_TM = 256
MOE_TM = 256
ROW_TM = 256


def _bdot(a, b):
    return jnp.dot(a.astype(BF16), b.astype(BF16), preferred_element_type=F32)


def _bdot_nt(a, b):
    return lax.dot_general(a.astype(BF16), b.astype(BF16), (((1,), (1,)), ((), ())),
                           preferred_element_type=F32)


def _bdot_tn(a, b):
    return lax.dot_general(a.astype(BF16), b.astype(BF16), (((0,), (0,)), ((), ())),
                           preferred_element_type=F32)


def _split(x):
    hi = x.astype(BF16)
    lo = (x - hi.astype(F32)).astype(BF16)
    return hi, lo


def _dot_exact_rhs(x, w_bf16):
    hi, lo = _split(x)
    return (jnp.dot(hi, w_bf16, preferred_element_type=F32)
            + jnp.dot(lo, w_bf16, preferred_element_type=F32))


def _dot3(a, b):
    ah, al = _split(a)
    bh, bl = _split(b)
    return (jnp.dot(ah, bh, preferred_element_type=F32)
            + jnp.dot(ah, bl, preferred_element_type=F32)
            + jnp.dot(al, bh, preferred_element_type=F32))


def _sigmoid(x):
    return 1.0 / (1.0 + jnp.exp(-x))


def _rms(x):
    return x * lax.rsqrt(jnp.mean(x * x, axis=-1, keepdims=True) + NORM_EPS)


def _cparams(*sem):
    return pltpu.CompilerParams(dimension_semantics=sem, vmem_limit_bytes=VMEM_LIMIT)


def _mod_kernel(c_ref, w_ref, b_ref, o_ref):
    c = c_ref[...]
    cs = c * _sigmoid(c)
    o_ref[...] = jnp.dot(cs, w_ref[...], preferred_element_type=F32, precision=HIGHEST) + b_ref[...]


def _mod_call(c_pad, w, b):
    rows, d = c_pad.shape
    cols = w.shape[1]
    blk = 1024
    return pl.pallas_call(
        _mod_kernel,
        out_shape=jax.ShapeDtypeStruct((rows, cols), F32),
        grid=(cols // blk,),
        in_specs=[pl.BlockSpec((rows, d), lambda j: (0, 0)),
                  pl.BlockSpec((d, blk), lambda j: (0, j)),
                  pl.BlockSpec((1, blk), lambda j: (0, j))],
        out_specs=pl.BlockSpec((rows, blk), lambda j: (0, j)),
        compiler_params=_cparams("arbitrary"),
        name="mod",
    )(c_pad, w, b)


RW_PCOLS = 3 * RW_WIDTH + 3 * LANES
NSA_MAIN = NSA_WIDTH + 6 * NSA_KV_WIDTH
NSA_PCOLS = NSA_MAIN + NSA_WIDTH + 2 * NSA_KV_WIDTH + NSA_KV_GROUPS * LANES


def _proj_kernel(x_ref, sc_ref, sh_ref, gmix_ref, wrw_ref, wn_ref, mu_ref, w0_ref, w2_ref,
                 a0_ref, a2_ref, g2_ref, kk_ref, ka_ref, ones_ref, cos_ref, sin_ref,
                 r_ref, ld_ref, k_ref, v_ref, na_ref, bb_ref, g_ref,
                 q_ref, kc_ref, vc_ref, ks_ref, vs_ref, kw_ref, vw_ref, gate_ref,
                 carry_ref):
    tm = x_ref.shape[1]
    t = pl.program_id(1)

    @pl.when(t == 0)
    def _():
        carry_ref[...] = jnp.zeros_like(carry_ref)

    x = x_ref[0]
    u = _rms(x) * gmix_ref[...] * (1.0 + sc_ref[0]) + sh_ref[0]
    ub = u.astype(BF16)

    zr = jnp.dot(ub, wrw_ref[...], preferred_element_type=F32)
    prev = pltpu.roll(zr, 1, axis=0)
    row = lax.broadcasted_iota(jnp.int32, zr.shape, 0)
    prev = jnp.where(row == 0, carry_ref[7:8, :], prev)
    carry_ref[...] = zr[tm - 8:tm, :]
    zr = zr + (prev - zr) * mu_ref[...]

    W = RW_WIDTH
    r = zr[:, 0:W]
    k = zr[:, W:2 * W]
    v = zr[:, 2 * W:3 * W]
    wl = zr[:, 3 * W:3 * W + LANES]
    al = zr[:, 3 * W + LANES:3 * W + 2 * LANES]
    gl = zr[:, 3 * W + 2 * LANES:3 * W + 3 * LANES]

    xw = -(w0_ref[...] + _bdot(jnp.tanh(wl), w2_ref[...]))
    softplus = jnp.maximum(xw, 0.0) + jnp.log1p(jnp.exp(-jnp.abs(xw)))
    w_log = -softplus - 0.5
    ld_ref[0] = -jnp.exp(w_log)
    a = _sigmoid(a0_ref[...] + _bdot(al, a2_ref[...]))
    g_ref[0] = _bdot(_sigmoid(gl), g2_ref[...])
    kk = k * kk_ref[...]
    ssq = _dot_exact_rhs(kk * kk, ones_ref[...])
    kk = kk / jnp.maximum(jnp.sqrt(ssq), 1e-12)
    r_ref[0] = r
    k_ref[0] = k * (1.0 + (a - 1.0) * ka_ref[...])
    v_ref[0] = v
    na_ref[0] = -kk
    bb_ref[0] = kk * a

    zn = jnp.dot(ub, wn_ref[...], preferred_element_type=F32)
    cos = cos_ref[...]
    sin = sin_ref[...]
    o = NSA_MAIN
    scale = NSA_HEAD_DIM ** -0.5
    for j in range(NSA_WIDTH // LANES):
        qs = zn[:, j * LANES:(j + 1) * LANES] * cos + zn[:, o + j * LANES:o + (j + 1) * LANES] * sin
        q_ref[0, :, j * LANES:(j + 1) * LANES] = (qs * scale).astype(q_ref.dtype)
    c0 = NSA_WIDTH
    kc_ref[0] = zn[:, c0:c0 + LANES]
    vc_ref[0] = zn[:, c0 + LANES:c0 + 2 * LANES]
    ks = zn[:, c0 + 2 * LANES:c0 + 3 * LANES] * cos + zn[:, o + NSA_WIDTH:o + NSA_WIDTH + LANES] * sin
    vs = zn[:, c0 + 3 * LANES:c0 + 4 * LANES]
    kw = zn[:, c0 + 4 * LANES:c0 + 5 * LANES] * cos + zn[:, o + NSA_WIDTH + LANES:o + NSA_WIDTH + 2 * LANES] * sin
    vw = zn[:, c0 + 5 * LANES:c0 + 6 * LANES]
    for gi in range(NSA_KV_GROUPS):
        sl = slice(gi * NSA_HEAD_DIM, (gi + 1) * NSA_HEAD_DIM)
        ks_ref[0, gi] = ks[:, sl].astype(ks_ref.dtype)
        vs_ref[0, gi] = vs[:, sl].astype(vs_ref.dtype)
        kw_ref[0, gi] = kw[:, sl].astype(kw_ref.dtype)
        vw_ref[0, gi] = vw[:, sl].astype(vw_ref.dtype)
    go = o + NSA_WIDTH + 2 * NSA_KV_WIDTH
    gate_ref[0] = _sigmoid(zn[:, go:go + NSA_KV_GROUPS * LANES])


def _proj_call(x, sc1, sh1, gmix, wrw, wn, mu, w0, w2, a0, a2, g2, kkp, kap, ones_bd, cos_t, sin_t):
    B, T, D = x.shape
    tm = PROJ_TM
    nt = T // tm
    G = NSA_KV_GROUPS
    tok = lambda w: pl.BlockSpec((1, tm, w), lambda b, t: (b, t, 0))
    full = lambda a: pl.BlockSpec(a.shape, lambda b, t: (0,) * a.ndim)
    perb = pl.BlockSpec((1, 1, D), lambda b, t: (b, 0, 0))
    kvspec = pl.BlockSpec((1, G, tm, NSA_HEAD_DIM), lambda b, t: (b, 0, t, 0))
    f32o = lambda w: jax.ShapeDtypeStruct((B, T, w), F32)
    kvo = jax.ShapeDtypeStruct((B, G, T, NSA_HEAD_DIM), BF16)
    out_shape = ([f32o(RW_WIDTH)] * 7
                 + [jax.ShapeDtypeStruct((B, T, NSA_WIDTH), BF16), f32o(LANES), f32o(LANES),
                    kvo, kvo, kvo, kvo, f32o(G * LANES)])
    out_specs = ([tok(RW_WIDTH)] * 7
                 + [tok(NSA_WIDTH), tok(LANES), tok(LANES), kvspec, kvspec, kvspec, kvspec, tok(G * LANES)])
    return pl.pallas_call(
        _proj_kernel,
        out_shape=out_shape,
        grid=(B, nt),
        in_specs=[tok(D), perb, perb, full(gmix), full(wrw), full(wn), full(mu), full(w0), full(w2),
                  full(a0), full(a2), full(g2), full(kkp), full(kap), full(ones_bd),
                  pl.BlockSpec((tm, LANES), lambda b, t: (t, 0)),
                  pl.BlockSpec((tm, LANES), lambda b, t: (t, 0))],
        out_specs=out_specs,
        scratch_shapes=[pltpu.VMEM((8, RW_PCOLS), F32)],
        compiler_params=_cparams("arbitrary", "arbitrary"),
        name="proj",
    )(x, sc1, sh1, gmix, wrw, wn, mu, w0, w2, a0, a2, g2, kkp, kap, ones_bd, cos_t, sin_t)


def _cumsum_rows(x):
    n = x.shape[0]
    row = lax.broadcasted_iota(jnp.int32, x.shape, 0)
    s = 1
    while s < n:
        x = x + jnp.where(row >= s, pltpu.roll(x, s, axis=0), 0.0)
        s *= 2
    return x


def _rwkv_kernel(r_ref, ld_ref, k_ref, v_ref, na_ref, bb_ref, g_ref, rk_ref, lnw_ref, lnb_ref,
                 o_ref, s_ref):
    C = RW_CHUNK
    N = RW_HEAD_DIM

    @pl.when(pl.program_id(1) == 0)
    def _():
        s_ref[...] = jnp.zeros_like(s_ref)

    ld = ld_ref[0]
    cum = _cumsum_rows(ld)
    tot = cum[C - 1:C, :]
    gq = jnp.exp(cum)
    ga = jnp.exp(cum - ld)
    gk = jnp.exp(-cum)
    ge = jnp.exp(tot - cum)
    gtot = jnp.exp(tot)
    r = r_ref[0]
    k = k_ref[0]
    v = v_ref[0]
    na = na_ref[0]
    bb = bb_ref[0]
    rt = r * gq
    at = na * ga
    bt = bb * gk
    kt = k * gk
    be = bb * ge
    ke = k * ge
    bonus_in = r * k * rk_ref[...]

    ri = lax.broadcasted_iota(jnp.int32, (C, C), 0)
    ci = lax.broadcasted_iota(jnp.int32, (C, C), 1)
    strict = ri > ci
    lower = ri >= ci
    eye = (ri == ci).astype(F32)

    outs = []
    for h in range(RW_HEADS):
        sl = slice(h * N, (h + 1) * N)
        ar = jnp.concatenate([at[:, sl], rt[:, sl]], axis=0)
        bk = jnp.concatenate([bt[:, sl], kt[:, sl]], axis=0)
        m = _bdot_nt(ar, bk)
        a_ab = jnp.where(strict, m[:C, :C], 0.0)
        a_ak = jnp.where(strict, m[:C, C:], 0.0)
        a_rb = jnp.where(lower, m[C:, :C], 0.0)
        a_rk = jnp.where(lower, m[C:, C:], 0.0)
        inv = eye + a_ab
        p = a_ab
        step = 2
        while step < C:
            p = _dot3(p, p)
            inv = inv + _dot3(inv, p)
            step *= 2
        s = s_ref[h]
        vh = v[:, sl]
        ars = _bdot_nt(ar, s)
        av = _bdot(jnp.concatenate([a_ak, a_rk], axis=0), vh)
        u = _dot3(inv, ars[:C] + av[:C])
        y = ars[C:] + av[C:] + _bdot(a_rb, u)
        uv = jnp.concatenate([u, vh], axis=0)
        bke = jnp.concatenate([be[:, sl], ke[:, sl]], axis=0)
        s_ref[h] = s * gtot[:, sl] + _bdot_tn(uv, bke)
        mean = jnp.mean(y, axis=-1, keepdims=True)
        var = jnp.mean(jnp.square(y - mean), axis=-1, keepdims=True)
        yn = (y - mean) * lax.rsqrt(var + RW_GN_EPS)
        bonus = jnp.sum(bonus_in[:, sl], axis=-1, keepdims=True) * vh
        outs.append((yn, bonus))
    yn = jnp.concatenate([o[0] for o in outs], axis=1)
    bonus = jnp.concatenate([o[1] for o in outs], axis=1)
    o_ref[0] = ((yn * lnw_ref[...] + lnb_ref[...] + bonus) * g_ref[0]).astype(o_ref.dtype)


def _rwkv_call(r, ld, k, v, na, bb, g, rk, lnw, lnb):
    B, T, W = r.shape
    C = RW_CHUNK
    tok = pl.BlockSpec((1, C, W), lambda b, t: (b, t, 0))
    par = pl.BlockSpec((1, W), lambda b, t: (0, 0))
    return pl.pallas_call(
        _rwkv_kernel,
        out_shape=jax.ShapeDtypeStruct((B, T, W), BF16),
        grid=(B, T // C),
        in_specs=[tok] * 7 + [par] * 3,
        out_specs=tok,
        scratch_shapes=[pltpu.VMEM((RW_HEADS, RW_HEAD_DIM, RW_HEAD_DIM), F32)],
        compiler_params=_cparams("arbitrary", "arbitrary"),
        name="rwkv",
    )(r, ld, k, v, na, bb, g, rk, lnw, lnb)


def _gelu_tanh(x):
    return 0.5 * x * (1.0 + jnp.tanh(math.sqrt(2.0 / math.pi) * (x + 0.044715 * (x * x * x))))


def _cmp_kernel(kc_ref, vc_ref, pek_ref, pev_ref, wak_ref, wbk_ref, wav_ref, wbv_ref,
                w2k_ref, w2kr_ref, w2v_ref, cos_ref, sin_ref, kcc_ref, vcc_ref):
    n = kc_ref.shape[1]

    def hidden(x, pe_ref, wa_ref, wb_ref):
        ha = _bdot(x + pe_ref[0:1, :], wa_ref[...])
        hb = _bdot(x + pe_ref[1:2, :], wb_ref[...])
        return _gelu_tanh(ha + pltpu.roll(hb, n - 1, axis=0))

    hk = hidden(kc_ref[0], pek_ref, wak_ref, wbk_ref)
    hv = hidden(vc_ref[0], pev_ref, wav_ref, wbv_ref)
    for gi in range(NSA_KV_GROUPS):
        sl = slice(gi * CMP_HIDDEN, (gi + 1) * CMP_HIDDEN)
        kcg = _bdot(hk[:, sl], w2k_ref[...]) * cos_ref[...] + _bdot(hk[:, sl], w2kr_ref[...]) * sin_ref[...]
        kcc_ref[0, gi] = kcg.astype(kcc_ref.dtype)
        vcc_ref[0, gi] = _bdot(hv[:, sl], w2v_ref[...]).astype(vcc_ref.dtype)


def _cmp_call(kc2, vc2, pek, pev, wak, wbk, wav, wbv, w2k, w2kr, w2v, cosc, sinc):
    B, n, wide = kc2.shape
    G = NSA_KV_GROUPS
    full = lambda a: pl.BlockSpec(a.shape, lambda b: (0,) * a.ndim)
    xin = pl.BlockSpec((1, n, wide), lambda b: (b, 0, 0))
    out = pl.BlockSpec((1, G, n, NSA_HEAD_DIM), lambda b: (b, 0, 0, 0))
    oshape = jax.ShapeDtypeStruct((B, G, n, NSA_HEAD_DIM), BF16)
    return pl.pallas_call(
        _cmp_kernel,
        out_shape=[oshape, oshape],
        grid=(B,),
        in_specs=[xin, xin] + [full(a) for a in (pek, pev, wak, wbk, wav, wbv, w2k, w2kr, w2v, cosc, sinc)],
        out_specs=[out, out],
        compiler_params=_cparams("arbitrary"),
        name="cmp",
    )(kc2, vc2, pek, pev, wak, wbk, wav, wbv, w2k, w2kr, w2v, cosc, sinc)


def _attn_kernel(q_ref, kcc_ref, vcc_ref, ks_ref, vs_ref, kw_ref, vw_ref, gate_ref, rmap_ref, e_ref,
                 o_ref):
    TQ = q_ref.shape[1]
    TK = ATTN_TK
    H = NSA_HPG
    DK = NSA_HEAD_DIM
    NC = kcc_ref.shape[2]
    NS = e_ref.shape[0]
    T = ks_ref.shape[2]
    q0 = pl.program_id(2) * TQ

    q = q_ref[0]
    qs = jnp.concatenate([q[:, h * DK:(h + 1) * DK] for h in range(H)], axis=0)
    rep = lambda m: jnp.concatenate([m] * H, axis=0)

    kcc = kcc_ref[0, 0]
    s = _bdot_nt(qs, kcc)
    qpos_c = q0 + lax.broadcasted_iota(jnp.int32, (TQ, NC), 0)
    cend = lax.broadcasted_iota(jnp.int32, (TQ, NC), 1) * CMP_STRIDE + (CMP_LEN - 1)
    mask_c = rep(cend <= qpos_c)
    s = jnp.where(mask_c, s, MASK_NEG)
    p = jnp.exp(s - jnp.max(s, axis=-1, keepdims=True)) * mask_c.astype(F32)
    den = jnp.sum(p, axis=-1, keepdims=True)
    p = p * jnp.where(den > 0.0, 1.0 / den, 0.0)
    o_c = _bdot(p, vcc_ref[0, 0])
    psum = p[0:TQ]
    for h in range(1, H):
        psum = psum + p[h * TQ:(h + 1) * TQ]

    p_slc = _dot_exact_rhs(psum, rmap_ref[...])
    sc = p_slc.T
    jblk = lax.broadcasted_iota(jnp.int32, (NS, TQ), 0)
    qblk = (q0 + lax.broadcasted_iota(jnp.int32, (NS, TQ), 1)) // SEL_BLK
    forced = (jblk == 0) | (jblk == qblk) | (jblk == qblk - 1)
    future = jblk > qblk
    sc = jnp.where(forced, SEL_FORCE, jnp.where(future, -SEL_FORCE, sc))
    rank = jnp.zeros((NS, TQ), F32)
    for j in range(NS):
        rowj = sc[j:j + 1, :]
        beats = (rowj > sc) | ((rowj == sc) & (jblk > j))
        rank = rank + beats.astype(F32)
    sel_t = ((rank < float(min(SEL_TOPN, NS))) & jnp.logical_not(future)).astype(F32)
    sel = sel_t.T.astype(BF16)

    qpos_k = q0 + lax.broadcasted_iota(jnp.int32, (TQ, TK), 0)
    lane_k = lax.broadcasted_iota(jnp.int32, (TQ, TK), 1)

    def sel_body(kt, carry):
        m_run, l_run, acc = carry
        k0 = pl.multiple_of(kt * TK, TK)
        kb = ks_ref[0, 0, pl.ds(k0, TK), :]
        vb = vs_ref[0, 0, pl.ds(k0, TK), :]
        s = _bdot_nt(qs, kb)
        picked = jnp.dot(sel, e_ref[:, pl.ds(k0, TK)], preferred_element_type=F32)
        allowed = rep((picked > 0.5) & (k0 + lane_k <= qpos_k))
        s = jnp.where(allowed, s, MASK_NEG)
        m_new = jnp.maximum(m_run, jnp.max(s, axis=-1, keepdims=True))
        alpha = jnp.exp(m_run - m_new)
        p = jnp.exp(s - m_new)
        l_new = alpha * l_run + jnp.sum(p, axis=-1, keepdims=True)
        acc = alpha * acc + _bdot(p, vb)
        return m_new, l_new, acc

    n_kt = (q0 + TQ + TK - 1) // TK
    init = (jnp.full((H * TQ, 1), MASK_NEG, F32), jnp.zeros((H * TQ, 1), F32), jnp.zeros((H * TQ, DK), F32))
    _, l_s, acc_s = lax.fori_loop(0, n_kt, sel_body, init)
    o_s = acc_s / l_s

    WK = WINDOW + TQ
    start = pl.multiple_of(jnp.maximum(q0 - WINDOW, 0), TQ)
    kwin = kw_ref[0, 0, pl.ds(start, WK), :]
    vwin = vw_ref[0, 0, pl.ds(start, WK), :]
    s = _bdot_nt(qs, kwin)
    qpos_w = q0 + lax.broadcasted_iota(jnp.int32, (TQ, WK), 0)
    kpos = start + lax.broadcasted_iota(jnp.int32, (TQ, WK), 1)
    mask_w = rep((kpos <= qpos_w) & (kpos > qpos_w - WINDOW))
    s = jnp.where(mask_w, s, MASK_NEG)
    p = jnp.exp(s - jnp.max(s, axis=-1, keepdims=True))
    o_w = _bdot(p, vwin) / jnp.sum(p, axis=-1, keepdims=True)

    gt = gate_ref[0]
    outs = []
    for h in range(H):
        rows = slice(h * TQ, (h + 1) * TQ)
        b0 = h * NSA_N_BRANCH
        outs.append(gt[:, b0:b0 + 1] * o_c[rows] + gt[:, b0 + 1:b0 + 2] * o_s[rows]
                    + gt[:, b0 + 2:b0 + 3] * o_w[rows])
    o_ref[0] = jnp.concatenate(outs, axis=1).astype(o_ref.dtype)


def _attn_call(q, kcc, vcc, ks, vs, kw, vw, gates, rmap, emat):
    B, T, _ = q.shape
    G = NSA_KV_GROUPS
    TQ = ATTN_TQ
    HW = NSA_HPG * NSA_HEAD_DIM
    NC = kcc.shape[2]
    qspec = pl.BlockSpec((1, TQ, HW), lambda b, g, t: (b, t, g))
    cspec = pl.BlockSpec((1, 1, NC, NSA_HEAD_DIM), lambda b, g, t: (b, g, 0, 0))
    kvspec = pl.BlockSpec((1, 1, T, NSA_HEAD_DIM), lambda b, g, t: (b, g, 0, 0))
    gspec = pl.BlockSpec((1, TQ, LANES), lambda b, g, t: (b, t, g))
    full = lambda a: pl.BlockSpec(a.shape, lambda b, g, t: (0,) * a.ndim)
    return pl.pallas_call(
        _attn_kernel,
        out_shape=jax.ShapeDtypeStruct((B, T, NSA_WIDTH), BF16),
        grid=(B, G, T // TQ),
        in_specs=[qspec, cspec, cspec, kvspec, kvspec, kvspec, kvspec, gspec, full(rmap), full(emat)],
        out_specs=qspec,
        compiler_params=_cparams("arbitrary", "arbitrary", "arbitrary"),
        name="attn",
    )(q, kcc, vcc, ks, vs, kw, vw, gates, rmap, emat)


def _merge_kernel(x_ref, yr_ref, yn_ref, sc1_ref, sh1_ref, gt1_ref, sc2_ref, sh2_ref, gmix_ref, gffn_ref,
                  wm_ref, pr_ref, pn_ref, wo_ref, wrt_ref, brt_ref,
                  h_ref, u2_ref, ids_ref, wts_ref, rank_ref, cnt_ref, base_ref):
    tm = x_ref.shape[1]
    D = x_ref.shape[2]
    first = (pl.program_id(0) == 0) & (pl.program_id(1) == 0)

    @pl.when(first)
    def _():
        base_ref[...] = jnp.zeros_like(base_ref)

    x = x_ref[0]
    u = _rms(x) * gmix_ref[...] * (1.0 + sc1_ref[0]) + sh1_ref[0]
    zg = jnp.dot(u.astype(BF16), wm_ref[...], preferred_element_type=F32)
    gates = _sigmoid(zg)
    merged = (gates[:, :D] * jnp.dot(yr_ref[0], pr_ref[...], preferred_element_type=F32)
              + gates[:, D:] * jnp.dot(yn_ref[0], pn_ref[...], preferred_element_type=F32))
    h = x + gt1_ref[0] * jnp.dot(merged.astype(BF16), wo_ref[...], preferred_element_type=F32)
    h_ref[0] = h
    u2 = _rms(h) * gffn_ref[...] * (1.0 + sc2_ref[0]) + sh2_ref[0]
    u2_ref[0] = u2

    logits = jnp.dot(u2, wrt_ref[...], preferred_element_type=F32, precision=HIGHEST) + brt_ref[...]
    lane = lax.broadcasted_iota(jnp.int32, logits.shape, 1)
    lane_f = lane.astype(F32)
    NG = N_EXPERT_GROUPS
    EPG = EXPERTS_PER_GROUP
    big = float(LANES)
    is_g = lane < NG
    gl = jnp.where(is_g, logits, MASK_NEG)
    gmax = jnp.max(gl, axis=-1, keepdims=True)
    grp = jnp.min(jnp.where(is_g & (gl == gmax), lane_f, big), axis=-1, keepdims=True)
    grp_w = 1.0 / jnp.sum(jnp.where(is_g, jnp.exp(gl - gmax), 0.0), axis=-1, keepdims=True)
    lo = float(NG) + grp * float(EPG)
    in_g = (lane_f >= lo) & (lane_f < lo + float(EPG))
    el = jnp.where(in_g, logits, MASK_NEG)
    v0 = jnp.max(el, axis=-1, keepdims=True)
    i0 = jnp.min(jnp.where(in_g & (el == v0), lane_f, big), axis=-1, keepdims=True)
    rest = in_g & (lane_f != i0)
    el1 = jnp.where(rest, logits, MASK_NEG)
    v1 = jnp.max(el1, axis=-1, keepdims=True)
    i1 = jnp.min(jnp.where(rest & (el1 == v1), lane_f, big), axis=-1, keepdims=True)
    e1w = jnp.exp(v1 - v0)
    c0 = grp_w / (1.0 + e1w)
    c1 = grp_w * e1w / (1.0 + e1w)
    e0 = i0 - float(NG)
    e1 = i1 - float(NG)
    slot = lax.broadcasted_iota(jnp.int32, (tm, 2), 1)
    ids_ref[0] = jnp.where(slot == 0, e0, e1).astype(jnp.int32)
    wts_ref[0] = jnp.where(slot == 0, c0, c1)

    oh0 = (lane_f == e0).astype(F32)
    oh1 = (lane_f == e1).astype(F32)
    both = oh0 + oh1
    ri = lax.broadcasted_iota(jnp.int32, (tm, tm), 0)
    ci = lax.broadcasted_iota(jnp.int32, (tm, tm), 1)
    tri = (ri > ci).astype(BF16)
    before = base_ref[...] + jnp.dot(tri, both.astype(BF16), preferred_element_type=F32)
    r0 = jnp.sum(oh0 * before, axis=-1, keepdims=True)
    r1 = jnp.sum(oh1 * before, axis=-1, keepdims=True)
    rank_ref[0] = jnp.where(slot == 0, r0, r1).astype(jnp.int32)
    total = base_ref[...] + jnp.sum(both, axis=0, keepdims=True)
    base_ref[...] = total
    cnt_ref[...] = total.astype(jnp.int32)


def _merge_call(x, yr, yn, sc1, sh1, gt1, sc2, sh2, gmix, gffn, wm, pr, pn, wo, wrt, brt):
    B, T, D = x.shape
    tm = MERGE_TM
    tok = lambda w: pl.BlockSpec((1, tm, w), lambda b, t: (b, t, 0))
    perb = pl.BlockSpec((1, 1, D), lambda b, t: (b, 0, 0))
    full = lambda a: pl.BlockSpec(a.shape, lambda b, t: (0,) * a.ndim)
    pair = pl.BlockSpec((1, tm, 2), lambda b, t: (b, t, 0))
    out_shape = [jax.ShapeDtypeStruct((B, T, D), F32), jax.ShapeDtypeStruct((B, T, D), F32),
                 jax.ShapeDtypeStruct((B, T, 2), jnp.int32), jax.ShapeDtypeStruct((B, T, 2), F32),
                 jax.ShapeDtypeStruct((B, T, 2), jnp.int32), jax.ShapeDtypeStruct((1, LANES), jnp.int32)]
    return pl.pallas_call(
        _merge_kernel,
        out_shape=out_shape,
        grid=(B, T // tm),
        in_specs=[tok(D), tok(RW_WIDTH), tok(NSA_WIDTH), perb, perb, perb, perb, perb, full(gmix), full(gffn),
                  full(wm), full(pr), full(pn), full(wo), full(wrt), full(brt)],
        out_specs=[tok(D), tok(D), pair, pair, pair, pl.BlockSpec((1, LANES), lambda b, t: (0, 0))],
        scratch_shapes=[pltpu.VMEM((1, LANES), F32)],
        compiler_params=_cparams("arbitrary", "arbitrary"),
        name="merge",
    )(x, yr, yn, sc1, sh1, gt1, sc2, sh2, gmix, gffn, wm, pr, pn, wo, wrt, brt)


def _dispatch_kernel(pos_ref, u_ref, xs_ref, sem):
    n = pos_ref.shape[2]

    def row_copy(j):
        return pltpu.make_async_copy(u_ref.at[pl.ds(j // 2, 1)], xs_ref.at[pl.ds(pos_ref[0, 0, j], 1)], sem)

    def start(j, c):
        row_copy(j).start()
        return c

    def wait(j, c):
        row_copy(j).wait()
        return c

    lax.fori_loop(0, n, start, 0)
    lax.fori_loop(0, n, wait, 0)


def _dispatch_call(pos, u2):
    N, D = u2.shape
    tm = ROW_TM
    pos2 = pos.reshape(N // tm, 1, 2 * tm)
    return pl.pallas_call(
        _dispatch_kernel,
        out_shape=jax.ShapeDtypeStruct((2 * N, D), u2.dtype),
        grid=(N // tm,),
        in_specs=[pl.BlockSpec((1, 1, 2 * tm), lambda i: (i, 0, 0), memory_space=pltpu.SMEM),
                  pl.BlockSpec((tm, D), lambda i: (i, 0))],
        out_specs=pl.BlockSpec(memory_space=pl.ANY),
        scratch_shapes=[pltpu.SemaphoreType.DMA],
        compiler_params=_cparams("arbitrary"),
        name="dispatch",
    )(pos2, u2)


def _expert_kernel(wt_ref, we_ref, wlo_ref, whi_ref, nw_ref, xs_ref, wg_ref, wu_ref, wd_ref, y_ref):
    w = pl.program_id(0)
    tm = xs_ref.shape[0]

    @pl.when(w < nw_ref[0])
    def _():
        xb = xs_ref[...].astype(BF16)
        gate = jnp.dot(xb, wg_ref[0], preferred_element_type=F32)
        up = jnp.dot(xb, wu_ref[0], preferred_element_type=F32)
        hid = gate * _sigmoid(gate) * up
        y = jnp.dot(hid.astype(BF16), wd_ref[0], preferred_element_type=F32)
        row = lax.broadcasted_iota(jnp.int32, y.shape, 0)
        y = jnp.where((row >= wlo_ref[w]) & (row < whi_ref[w]), y, 0.0)
        prev_tile = wt_ref[jnp.maximum(w - 1, 0)]
        first = (w == 0) | (prev_tile != wt_ref[w])

        @pl.when(first)
        def _():
            y_ref[...] = y

        @pl.when(jnp.logical_not(first))
        def _():
            y_ref[...] += y


def _expert_call(w_tile, w_exp, w_lo, w_hi, n_work, xs, wg, wu, wd):
    P, D = xs.shape
    tm = MOE_TM
    E, _, Hd = wg.shape
    n_items = w_tile.shape[0]
    grid_spec = pltpu.PrefetchScalarGridSpec(
        num_scalar_prefetch=5,
        grid=(n_items,),
        in_specs=[pl.BlockSpec((tm, D), lambda w, wt, we, lo, hi, nw: (wt[w], 0)),
                  pl.BlockSpec((1, D, Hd), lambda w, wt, we, lo, hi, nw: (we[w], 0, 0)),
                  pl.BlockSpec((1, D, Hd), lambda w, wt, we, lo, hi, nw: (we[w], 0, 0)),
                  pl.BlockSpec((1, Hd, D), lambda w, wt, we, lo, hi, nw: (we[w], 0, 0))],
        out_specs=pl.BlockSpec((tm, D), lambda w, wt, we, lo, hi, nw: (wt[w], 0)),
    )
    return pl.pallas_call(
        _expert_kernel,
        out_shape=jax.ShapeDtypeStruct((P, D), F32),
        grid_spec=grid_spec,
        compiler_params=_cparams("arbitrary"),
        name="expert",
    )(w_tile, w_exp, w_lo, w_hi, n_work, xs, wg, wu, wd)


def _work_list(counts, n_rows, tm):
    E = counts.shape[0]
    n_tiles = n_rows // tm
    n_items = n_tiles + E - 1
    ends = jnp.cumsum(counts)
    starts = ends - counts
    first_tile = starts // tm
    last_tile = jnp.maximum(ends - 1, 0) // tm
    items = jnp.where(counts > 0, last_tile - first_tile + 1, 0)
    item_end = jnp.cumsum(items)
    item_start = item_end - items
    n_work = item_end[-1]
    w = jnp.minimum(jnp.arange(n_items, dtype=jnp.int32), n_work - 1)
    e = jnp.minimum(jnp.searchsorted(item_end, w, side='right'), E - 1).astype(jnp.int32)
    tile = first_tile[e] + (w - item_start[e])
    lo = jnp.maximum(starts[e], tile * tm) - tile * tm
    hi = jnp.minimum(ends[e], (tile + 1) * tm) - tile * tm
    i32 = lambda a: a.astype(jnp.int32)
    return i32(tile), i32(e), i32(lo), i32(hi), i32(n_work).reshape(1), i32(starts)


def _combine_kernel(pos_ref, h_ref, gt2_ref, wts_ref, gfin_ref, yb_ref, o_ref, buf_ref, sem):
    n = pos_ref.shape[2]

    def row_copy(j):
        return pltpu.make_async_copy(yb_ref.at[pl.ds(pos_ref[0, 0, j], 1)], buf_ref.at[j % 2, pl.ds(j // 2, 1)],
                                     sem)

    def start(j, c):
        row_copy(j).start()
        return c

    def wait(j, c):
        row_copy(j).wait()
        return c

    lax.fori_loop(0, n, start, 0)
    lax.fori_loop(0, n, wait, 0)
    wts = wts_ref[...]
    moe = wts[:, 0:1] * buf_ref[0] + wts[:, 1:2] * buf_ref[1]
    h = h_ref[...] + gt2_ref[0] * moe
    o_ref[...] = _rms(h) * gfin_ref[...]


def _combine_call(pos, h, gt2, wts, gfin, yb, tokens_per_batch):
    N, D = h.shape
    tm = ROW_TM
    per = tokens_per_batch // tm
    pos2 = pos.reshape(N // tm, 1, 2 * tm)
    return pl.pallas_call(
        _combine_kernel,
        out_shape=jax.ShapeDtypeStruct((N, D), F32),
        grid=(N // tm,),
        in_specs=[pl.BlockSpec((1, 1, 2 * tm), lambda i: (i, 0, 0), memory_space=pltpu.SMEM),
                  pl.BlockSpec((tm, D), lambda i: (i, 0)),
                  pl.BlockSpec((1, 1, D), lambda i: (i // per, 0, 0)),
                  pl.BlockSpec((tm, 2), lambda i: (i, 0)),
                  pl.BlockSpec((1, D), lambda i: (0, 0)),
                  pl.BlockSpec(memory_space=pl.ANY)],
        out_specs=pl.BlockSpec((tm, D), lambda i: (i, 0)),
        scratch_shapes=[pltpu.VMEM((2, tm, D), F32), pltpu.SemaphoreType.DMA],
        compiler_params=_cparams("arbitrary"),
        name="combine",
    )(pos2, h, gt2, wts, gfin, yb)


def _rot_cols(w):
    half = NSA_HEAD_DIM // 2
    k, n = w.shape
    w3 = w.reshape(k, n // NSA_HEAD_DIM, NSA_HEAD_DIM)
    return jnp.concatenate([-w3[..., half:], w3[..., :half]], axis=-1).reshape(k, n)


def _rope_tables(pos):
    half = NSA_HEAD_DIM // 2
    inv = jnp.exp(-math.log(ROPE_THETA) * jnp.arange(half, dtype=F32) / half)
    ang = pos.astype(F32)[:, None] * inv[None, :]
    cos = jnp.cos(ang)
    sin = jnp.sin(ang)
    return jnp.concatenate([cos, cos], axis=1), jnp.concatenate([sin, sin], axis=1)


def _pad_cols(w, n):
    return jnp.pad(w, ((0, 0), (0, n - w.shape[1])))


def _pad_rows(w, n):
    return jnp.pad(w, ((0, n - w.shape[0]), (0, 0)))


def _cmp_weights(w1):
    G, DK, F = NSA_KV_GROUPS, NSA_HEAD_DIM, CMP_HIDDEN
    half = CMP_LEN // 2
    w = w1.reshape(CMP_LEN, DK, F)
    eye = jnp.eye(G, dtype=w1.dtype)

    def expand(wh):
        return jnp.einsum('ldf,gh->lgdhf', wh, eye).reshape(half * G * DK, G * F)

    return expand(w[:half]).astype(BF16), expand(w[half:]).astype(BF16)


def _cmp_pe(pe):
    G = NSA_KV_GROUPS
    half = CMP_LEN // 2
    rows = [jnp.broadcast_to(pe[i * half:(i + 1) * half, None, :], (half, G, NSA_HEAD_DIM)).reshape(1, -1)
            for i in range(2)]
    return jnp.concatenate(rows, axis=0)


def _remap_matrix(n_cmp_pad, n_sel):
    sel_ratio = SEL_BLK // CMP_STRIDE
    cmp_ratio = CMP_LEN // CMP_STRIDE
    wts = np.convolve(np.ones(sel_ratio), np.ones(cmp_ratio))
    m = np.zeros((n_cmp_pad, n_sel), np.float32)
    for j in range(n_sel):
        for s in range(sel_ratio + cmp_ratio - 1):
            src = j * sel_ratio + s - (cmp_ratio - 1)
            if 0 <= src < n_cmp_pad:
                m[src, j] = wts[s]
    return jnp.asarray(m, BF16)


def _block_expand_matrix(n_sel, t):
    e = (np.arange(t)[None, :] // SEL_BLK == np.arange(n_sel)[:, None]).astype(np.float32)
    return jnp.asarray(e, BF16)


def kernel(x, c, w_ada, b_ada, norm_mix_g, w_in, rwkv_mu, rwkv_w0, rwkv_w2, rwkv_a0, rwkv_a2, rwkv_g2,
           rwkv_kk, rwkv_ka, rwkv_rk, rwkv_ln_w, rwkv_ln_b, cmp_pe_k, cmp_w1_k, cmp_w2_k, cmp_pe_v,
           cmp_w1_v, cmp_w2_v, p_rwkv, p_nsa, w_out, norm_ffn_g, w_router_grp, b_router_grp,
           w_router_exp, b_router_exp, moe_w_gate, moe_w_up, moe_w_down, norm_final_g):
    B, T, D = x.shape
    N = B * T
    depth = w_ada.shape[0]
    W = RW_WIDTH
    G = NSA_KV_GROUPS
    row = lambda a: a.reshape(1, -1)
    pos = jnp.arange(T)
    cos_t, sin_t = _rope_tables(pos)
    cos_t = jnp.concatenate([cos_t, cos_t], axis=1)
    sin_t = jnp.concatenate([sin_t, sin_t], axis=1)
    n_cmp_pad = T // CMP_STRIDE
    cosc, sinc = _rope_tables(jnp.arange(n_cmp_pad) * CMP_STRIDE + CMP_LEN - 1)
    n_sel = T // SEL_BLK
    rmap = _remap_matrix(n_cmp_pad, n_sel)
    emat = _block_expand_matrix(n_sel, T)
    ones_bd = jnp.asarray(np.kron(np.eye(RW_HEADS), np.ones((RW_HEAD_DIM, RW_HEAD_DIM))), BF16)
    c_pad = jnp.pad(c, ((0, 8 - B), (0, 0)))

    h = x
    for l in range(depth):
        mod = _mod_call(c_pad, w_ada[l], row(b_ada[l]))[:B]
        sh1, sc1, gt1, sh2, sc2, gt2 = [m.reshape(B, 1, D) for m in jnp.split(mod, 6, axis=-1)]

        wi = w_in[l]
        o = 0
        w_r, w_k, w_v = wi[:, 0:W], wi[:, W:2 * W], wi[:, 2 * W:3 * W]
        o = 3 * W
        w_wl = wi[:, o:o + RW_DECAY_LORA]; o += RW_DECAY_LORA
        w_al = wi[:, o:o + RW_AAA_LORA]; o += RW_AAA_LORA
        w_gl = wi[:, o:o + RW_GATE_LORA]; o += RW_GATE_LORA
        wrw = jnp.concatenate([w_r, w_k, w_v, _pad_cols(w_wl, LANES), _pad_cols(w_al, LANES), w_gl],
                              axis=1).astype(BF16)
        mu = rwkv_mu[l]
        mu_p = jnp.concatenate([mu[:3 * W],
                                jnp.pad(mu[3 * W:3 * W + RW_DECAY_LORA], (0, LANES - RW_DECAY_LORA)),
                                jnp.pad(mu[3 * W + RW_DECAY_LORA:3 * W + RW_DECAY_LORA + RW_AAA_LORA],
                                        (0, LANES - RW_AAA_LORA)),
                                mu[3 * W + RW_DECAY_LORA + RW_AAA_LORA:]]).reshape(1, -1)
        w_q = wi[:, o:o + NSA_WIDTH]; o += NSA_WIDTH
        kvw = []
        for _ in range(6):
            kvw.append(wi[:, o:o + NSA_KV_WIDTH]); o += NSA_KV_WIDTH
        w_kc, w_vc, w_ks, w_vs, w_kw, w_vw = kvw
        w_gate = wi[:, o:o + NSA_N_BRANCH * NSA_HEADS]; o += NSA_N_BRANCH * NSA_HEADS
        per_g = NSA_N_BRANCH * NSA_HPG
        w_gate_p = jnp.concatenate([_pad_cols(w_gate[:, gi * per_g:(gi + 1) * per_g], LANES) for gi in range(G)],
                                   axis=1)
        wn = jnp.concatenate([w_q, w_kc, w_vc, w_ks, w_vs, w_kw, w_vw,
                              _rot_cols(w_q), _rot_cols(w_ks), _rot_cols(w_kw), w_gate_p], axis=1).astype(BF16)
        w_m = wi[:, o:o + 2 * D].astype(BF16)

        (r, ld, k, v, na, bb, g, q, kc, vc, ks, vs, kw, vw, gates) = _proj_call(
            h, sc1, sh1, row(norm_mix_g[l]), wrw, wn, mu_p, row(rwkv_w0[l]),
            _pad_rows(rwkv_w2[l], LANES).astype(BF16), row(rwkv_a0[l]),
            _pad_rows(rwkv_a2[l], LANES).astype(BF16), rwkv_g2[l].astype(BF16),
            row(rwkv_kk[l]), row(rwkv_ka[l]), ones_bd, cos_t, sin_t)

        y_r = _rwkv_call(r, ld, k, v, na, bb, g, row(rwkv_rk[l]), row(rwkv_ln_w[l]), row(rwkv_ln_b[l]))

        wak, wbk = _cmp_weights(cmp_w1_k[l])
        wav, wbv = _cmp_weights(cmp_w1_v[l])
        kcc, vcc = _cmp_call(kc.reshape(B, n_cmp_pad, CMP_STRIDE * G * NSA_HEAD_DIM),
                             vc.reshape(B, n_cmp_pad, CMP_STRIDE * G * NSA_HEAD_DIM),
                             _cmp_pe(cmp_pe_k[l]), _cmp_pe(cmp_pe_v[l]), wak, wbk, wav, wbv,
                             cmp_w2_k[l].astype(BF16), _rot_cols(cmp_w2_k[l]).astype(BF16),
                             cmp_w2_v[l].astype(BF16), cosc, sinc)
        y_n = _attn_call(q, kcc, vcc, ks, vs, kw, vw, gates, rmap, emat)

        wrt = _pad_cols(jnp.concatenate([w_router_grp[l], w_router_exp[l]], axis=1), LANES)
        brt = _pad_cols(jnp.concatenate([b_router_grp[l], b_router_exp[l]]).reshape(1, -1), LANES)
        h, u2, ids, wts, rank, counts = _merge_call(
            h, y_r, y_n, sc1, sh1, gt1, sc2, sh2, row(norm_mix_g[l]), row(norm_ffn_g[l]), w_m,
            p_rwkv[l].astype(BF16), p_nsa[l].astype(BF16), w_out[l].astype(BF16), wrt, brt)

        w_tile, w_exp, w_lo, w_hi, n_work, starts = _work_list(counts[0, :N_EXPERTS], 2 * N, MOE_TM)
        ids = ids.reshape(N, 2)
        posn = (starts[ids] + rank.reshape(N, 2)).reshape(-1)
        xs = _dispatch_call(posn, u2.reshape(N, D))
        yb = _expert_call(w_tile, w_exp, w_lo, w_hi, n_work, xs, moe_w_gate[l].astype(BF16),
                          moe_w_up[l].astype(BF16), moe_w_down[l].astype(BF16))
        final = l == depth - 1
        gfin = row(norm_final_g) if final else jnp.ones((1, D), F32)
        out = _combine_call(posn, h.reshape(N, D), gt2, wts.reshape(N, 2), gfin, yb, T)
        h = out.reshape(B, T, D)
    return h
```
